```python
import jax, jax.numpy as jnp
from jax import lax
import numpy as np

D_MODEL = 1024
BATCH = 4
SEQ = 4096
DEPTH = 4

LRU_WIDTH = 512
LRU_BLOCKS = 8
LRU_BLOCK = LRU_WIDTH // LRU_BLOCKS
CONV_WIDTH = 4
LRU_C = 8.0
MLA_HEADS = 4
QK_NOPE = 128
QK_ROPE = 64
QK_HEAD = QK_NOPE + QK_ROPE
V_HEAD = 128
Q_LORA = 768
KV_LORA = 256
MLA_WIDTH = MLA_HEADS * V_HEAD
MIX_WIDTH = LRU_WIDTH + MLA_WIDTH
IN_WIDTH = 2 * LRU_WIDTH + Q_LORA + KV_LORA + QK_ROPE
ROPE_BASE = 10000.0
Q_BLOCK = 128
N_EXPERTS = 16
N_GROUPS = 4
EXPERTS_PER_GROUP = N_EXPERTS // N_GROUPS
TOP_K = 2
D_EXPERT = 512
MOE_BLOCK = 128
DN_ALPHA = (2 * DEPTH) ** 0.25
DN_BETA = (8 * DEPTH) ** -0.25
EPS = 1e-6

kernel_name = 'hybrid_rglru_mla_moe_deepnorm'


def layer_norm(x, g, b):
    xf = x.astype(jnp.float32)
    mu = jnp.mean(xf, axis=-1, keepdims=True)
    var = jnp.mean(jnp.square(xf - mu), axis=-1, keepdims=True)
    return ((xf - mu) * lax.rsqrt(var + EPS) * g.astype(jnp.float32) + b.astype(jnp.float32)).astype(x.dtype)


def rms_norm(x, g):
    xf = x.astype(jnp.float32)
    ms = jnp.mean(xf * xf, axis=-1, keepdims=True)
    return (xf * lax.rsqrt(ms + EPS) * g.astype(jnp.float32)).astype(x.dtype)


def rope(x, pos):
    half = QK_ROPE // 2
    inv = ROPE_BASE ** (-jnp.arange(half, dtype=jnp.float32) / half)
    ang = pos.astype(jnp.float32)[:, None] * inv[None, :]
    cos = jnp.cos(ang)[None, :, None, :]
    sin = jnp.sin(ang)[None, :, None, :]
    xf = x.astype(jnp.float32)
    x1, x2 = xf[..., :half], xf[..., half:]
    return jnp.concatenate([x1 * cos - x2 * sin, x1 * sin + x2 * cos], axis=-1).astype(x.dtype)


def rg_lru_group(u, gate_in, conv_w, conv_b, w_rgate, b_rgate, w_igate, b_igate, lru_lambda):
    bsz, seq, _ = u.shape
    xc = lax.conv_general_dilated(u, conv_w[:, None, :], window_strides=(1,),
                                  padding=[(CONV_WIDTH - 1, 0)],
                                  dimension_numbers=('NWC', 'WIO', 'NWC'),
                                  feature_group_count=LRU_WIDTH) + conv_b
    xb = xc.reshape(bsz, seq, LRU_BLOCKS, LRU_BLOCK)
    r = jax.nn.sigmoid(jnp.einsum('bsnc,ncd->bsnd', xb, w_rgate) + b_rgate).reshape(bsz, seq, LRU_WIDTH)
    i = jax.nn.sigmoid(jnp.einsum('bsnc,ncd->bsnd', xb, w_igate) + b_igate).reshape(bsz, seq, LRU_WIDTH)
    log_a = (-LRU_C * r.astype(jnp.float32)) * jax.nn.softplus(-lru_lambda.astype(jnp.float32))
    a = jnp.exp(log_a)
    drive = jnp.sqrt(-jnp.expm1(2.0 * log_a)) * (i * xc).astype(jnp.float32)

    def combine(left, right):
        a1, b1 = left
        a2, b2 = right
        return a1 * a2, a2 * b1 + b2

    _, h = lax.associative_scan(combine, (a, drive), axis=1)
    return h.astype(u.dtype) * jax.nn.gelu(gate_in)


def causal_block_attention(q, k, v):
    bsz, seq, nh, dk = q.shape
    nb = seq // Q_BLOCK
    qb = q.reshape(bsz, nb, Q_BLOCK, nh, dk).transpose(1, 0, 2, 3, 4)
    k_pos = jnp.arange(seq)

    def one_block(args):
        qi, bi = args
        s = jnp.einsum('bqhd,bkhd->bhqk', qi, k, preferred_element_type=jnp.float32)
        q_pos = bi * Q_BLOCK + jnp.arange(Q_BLOCK)
        s = jnp.where(k_pos[None, :] <= q_pos[:, None], s, -jnp.inf)
        p = jax.nn.softmax(s, axis=-1)
        return jnp.einsum('bhqk,bkhd->bqhd', p.astype(v.dtype), v)

    out = lax.map(one_block, (qb, jnp.arange(nb)))
    return out.transpose(1, 0, 2, 3, 4).reshape(bsz, seq, nh * V_HEAD)


def mla_group(q_lat, kv_lat, k_rope_raw, q_norm_g, w_q_up, kv_norm_g, w_kv_up, pos):
    bsz, seq, _ = q_lat.shape
    q = jnp.einsum('bsr,rhd->bshd', rms_norm(q_lat, q_norm_g), w_q_up)
    q = jnp.concatenate([q[..., :QK_NOPE], rope(q[..., QK_NOPE:], pos)], axis=-1)
    kv = jnp.einsum('bsr,rhd->bshd', rms_norm(kv_lat, kv_norm_g), w_kv_up)
    k_nope, v = kv[..., :QK_NOPE], kv[..., QK_NOPE:]
    k_pe = rope(k_rope_raw[:, :, None, :], pos)
    k = jnp.concatenate([k_nope, jnp.broadcast_to(k_pe, (bsz, seq, MLA_HEADS, QK_ROPE))], axis=-1)
    return causal_block_attention(q * (QK_HEAD ** -0.5), k, v)


def route(x2, router_w, router_bias):
    n_tok = x2.shape[0]
    s = jax.nn.sigmoid(jnp.einsum('nd,de->ne', x2, router_w, preferred_element_type=jnp.float32))
    s_sel = (s + router_bias.astype(jnp.float32)).reshape(n_tok, N_GROUPS, EXPERTS_PER_GROUP)
    grp_score = jnp.sum(lax.top_k(s_sel, 2)[0], axis=-1)
    best_g = jnp.argmax(grp_score, axis=-1)
    _, local = lax.top_k(s_sel[jnp.arange(n_tok), best_g], TOP_K)
    expert_idx = best_g[:, None] * EXPERTS_PER_GROUP + local
    w = jnp.take_along_axis(s, expert_idx, axis=1)
    return expert_idx, w / jnp.sum(w, axis=-1, keepdims=True)


def moe_ffn(x2, router_w, router_bias, w_exp_gate, w_exp_up, w_exp_down):
    n_tok, d = x2.shape
    n_asg = n_tok * TOP_K
    expert_idx, gates = route(x2, router_w, router_bias)
    flat_e = expert_idx.reshape(-1)
    order = jnp.argsort(flat_e)
    sorted_e = flat_e[order]
    tok = order // TOP_K
    counts = jnp.bincount(flat_e, length=N_EXPERTS)
    padded = (counts + MOE_BLOCK - 1) // MOE_BLOCK * MOE_BLOCK
    pad_end = jnp.cumsum(padded)
    pad_start = pad_end - padded
    start = jnp.cumsum(counts) - counts
    dest = pad_start[sorted_e] + (jnp.arange(n_asg) - start[sorted_e])
    n_blocks = -(-n_asg // MOE_BLOCK) + N_EXPERTS
    buf = jnp.zeros((n_blocks * MOE_BLOCK, d), x2.dtype).at[dest].set(x2[tok])
    block_expert = jnp.minimum(jnp.searchsorted(pad_end, jnp.arange(n_blocks) * MOE_BLOCK, side='right'),
                               N_EXPERTS - 1)

    def expert_block(args):
        xb, e = args
        hb = jax.nn.silu(xb @ w_exp_gate[e]) * (xb @ w_exp_up[e])
        return hb @ w_exp_down[e]

    yb = lax.map(expert_block, (buf.reshape(n_blocks, MOE_BLOCK, d), block_expert))
    y_sorted = yb.reshape(n_blocks * MOE_BLOCK, d)[dest]
    g_sorted = gates.reshape(-1)[order].astype(x2.dtype)
    return jax.ops.segment_sum(y_sorted * g_sorted[:, None], tok, num_segments=n_tok)


def hybrid_layer(x, w_in, conv_w, conv_b, w_rgate, b_rgate, w_igate, b_igate, lru_lambda,
                 q_norm_g, w_q_up, kv_norm_g, w_kv_up, out_norm_g, w_out, ln1_g, ln1_b,
                 router_w, router_bias, w_exp_gate, w_exp_up, w_exp_down, ln2_g, ln2_b):
    bsz, seq, d = x.shape
    pos = jnp.arange(seq)
    z = jnp.einsum('bsd,dc->bsc', x, w_in)
    o1 = LRU_WIDTH
    o2 = 2 * LRU_WIDTH
    o3 = o2 + Q_LORA
    o4 = o3 + KV_LORA
    u, gate_in, q_lat, kv_lat, k_rope_raw = jnp.split(z, [o1, o2, o3, o4], axis=-1)
    lru_out = rg_lru_group(u, gate_in, conv_w, conv_b, w_rgate, b_rgate, w_igate, b_igate, lru_lambda)
    mla_out = mla_group(q_lat, kv_lat, k_rope_raw, q_norm_g, w_q_up, kv_norm_g, w_kv_up, pos)
    mix = jnp.concatenate([rms_norm(lru_out, out_norm_g[:LRU_WIDTH]),
                           rms_norm(mla_out, out_norm_g[LRU_WIDTH:])], axis=-1)
    h = jnp.einsum('bsc,cd->bsd', mix, w_out)
    x = layer_norm(DN_ALPHA * x + h, ln1_g, ln1_b)
    f = moe_ffn(x.reshape(bsz * seq, d), router_w, router_bias, w_exp_gate, w_exp_up, w_exp_down)
    return layer_norm(DN_ALPHA * x + f.reshape(bsz, seq, d), ln2_g, ln2_b)


def setup_inputs(seed: int = 0) -> dict:
    key = jax.random.key(seed)
    ks = jax.random.split(key, 24)
    L, D = DEPTH, D_MODEL
    nrm = lambda k, shape, scale: jax.random.normal(k, shape, jnp.float32) * scale
    x = jax.random.normal(ks[0], (BATCH, SEQ, D), jnp.float32)
    w_in = nrm(ks[1], (L, D, IN_WIDTH), D ** -0.5)
    conv_w = nrm(ks[2], (L, CONV_WIDTH, LRU_WIDTH), CONV_WIDTH ** -0.5)
    conv_b = nrm(ks[3], (L, LRU_WIDTH), 0.02)
    w_rgate = nrm(ks[4], (L, LRU_BLOCKS, LRU_BLOCK, LRU_BLOCK), LRU_BLOCK ** -0.5)
    b_rgate = nrm(ks[5], (L, LRU_BLOCKS, LRU_BLOCK), 0.02)
    w_igate = nrm(ks[6], (L, LRU_BLOCKS, LRU_BLOCK, LRU_BLOCK), LRU_BLOCK ** -0.5)
    b_igate = nrm(ks[7], (L, LRU_BLOCKS, LRU_BLOCK), 0.02)
    a0 = jax.random.uniform(ks[8], (L, LRU_WIDTH), jnp.float32, 0.9, 0.999)
    lru_lambda = jnp.log(a0) - jnp.log1p(-a0)
    q_norm_g = 1.0 + nrm(ks[9], (L, Q_LORA), 0.02)
    w_q_up = nrm(ks[10], (L, Q_LORA, MLA_HEADS, QK_HEAD), Q_LORA ** -0.5)
    kv_norm_g = 1.0 + nrm(ks[11], (L, KV_LORA), 0.02)
    kv_scale = jnp.concatenate([jnp.ones((QK_NOPE,), jnp.float32), jnp.full((V_HEAD,), DN_BETA, jnp.float32)])
    w_kv_up = nrm(ks[12], (L, KV_LORA, MLA_HEADS, QK_NOPE + V_HEAD), KV_LORA ** -0.5) * kv_scale
    out_norm_g = 1.0 + nrm(ks[13], (L, MIX_WIDTH), 0.02)
    w_out = nrm(ks[14], (L, MIX_WIDTH, D), MIX_WIDTH ** -0.5 * DN_BETA)
    ln1_g = 1.0 + nrm(ks[15], (L, D), 0.02)
    ln1_b = nrm(ks[16], (L, D), 0.02)
    router_w = nrm(ks[17], (D, N_EXPERTS), D ** -0.5)
    router_bias = nrm(ks[18], (N_EXPERTS,), 0.01)
    w_exp_gate = nrm(ks[19], (L, N_EXPERTS, D, D_EXPERT), D ** -0.5)
    w_exp_up = nrm(ks[20], (L, N_EXPERTS, D, D_EXPERT), D ** -0.5)
    w_exp_down = nrm(ks[21], (L, N_EXPERTS, D_EXPERT, D), D_EXPERT ** -0.5 * DN_BETA)
    ln2_g = 1.0 + nrm(ks[22], (L, D), 0.02)
    ln2_b = nrm(ks[23], (L, D), 0.02)
    return {'x': x, 'w_in': w_in, 'conv_w': conv_w, 'conv_b': conv_b,
            'w_rgate': w_rgate, 'b_rgate': b_rgate, 'w_igate': w_igate, 'b_igate': b_igate,
            'lru_lambda': lru_lambda, 'q_norm_g': q_norm_g, 'w_q_up': w_q_up,
            'kv_norm_g': kv_norm_g, 'w_kv_up': w_kv_up, 'out_norm_g': out_norm_g, 'w_out': w_out,
            'ln1_g': ln1_g, 'ln1_b': ln1_b, 'router_w': router_w, 'router_bias': router_bias,
            'w_exp_gate': w_exp_gate, 'w_exp_up': w_exp_up, 'w_exp_down': w_exp_down,
            'ln2_g': ln2_g, 'ln2_b': ln2_b}


def reference(x, w_in, conv_w, conv_b, w_rgate, b_rgate, w_igate, b_igate, lru_lambda,
              q_norm_g, w_q_up, kv_norm_g, w_kv_up, out_norm_g, w_out, ln1_g, ln1_b,
              router_w, router_bias, w_exp_gate, w_exp_up, w_exp_down, ln2_g, ln2_b):
    for l in range(DEPTH):
        x = hybrid_layer(x, w_in[l], conv_w[l], conv_b[l], w_rgate[l], b_rgate[l], w_igate[l], b_igate[l],
                         lru_lambda[l], q_norm_g[l], w_q_up[l], kv_norm_g[l], w_kv_up[l], out_norm_g[l],
                         w_out[l], ln1_g[l], ln1_b[l], router_w, router_bias,
                         w_exp_gate[l], w_exp_up[l], w_exp_down[l], ln2_g[l], ln2_b[l])
    return x
```

```python
import functools

import jax
import jax.numpy as jnp
from jax import lax
from jax.experimental import pallas as pl
from jax.experimental.pallas import tpu as pltpu

D_MODEL = 1024
LRU_WIDTH = 512
LRU_BLOCKS = 8
CONV_WIDTH = 4
LRU_C = 8.0
MLA_HEADS = 4
QK_NOPE = 128
QK_ROPE = 64
QK_HEAD = QK_NOPE + QK_ROPE
V_HEAD = 128
Q_LORA = 768
KV_LORA = 256
MLA_WIDTH = MLA_HEADS * V_HEAD
ROPE_BASE = 10000.0
N_EXPERTS = 16
N_GROUPS = 4
EXPERTS_PER_GROUP = N_EXPERTS // N_GROUPS
TOP_K = 2
D_EXPERT = 512
EPS = 1e-6

LANES = 128
SUBLANES = 8
VMEM_LIMIT = 56 * 1024 * 1024

ROPE_PAIR = 2 * QK_ROPE
IN_COLS = 2 * LRU_WIDTH + Q_LORA + KV_LORA + 2 * ROPE_PAIR
Q_COLS = MLA_HEADS * QK_NOPE + MLA_HEADS * QK_ROPE
K_COLS = MLA_HEADS * QK_NOPE + 2 * ROPE_PAIR
HALF = D_MODEL // 2

TM = 512
TQ = 512
TK = 512
TE = 256

F32 = jnp.float32
BF16 = jnp.bfloat16
NEG_BIG = -1e30


def _rms(x, g):
    ms = jnp.mean(x * x, axis=-1, keepdims=True)
    return x * lax.rsqrt(ms + EPS) * g


def _layer_norm(x, g, b):
    mu = jnp.mean(x, axis=-1, keepdims=True)
    xc = x - mu
    var = jnp.mean(xc * xc, axis=-1, keepdims=True)
    return xc * lax.rsqrt(var + EPS) * g + b


def _sigmoid(x):
    return 1.0 / (1.0 + jnp.exp(-x))


def _pack_halves(x):
    lo = pltpu.bitcast(x[:, :HALF].astype(BF16).astype(F32), jnp.uint32)
    hi = pltpu.bitcast(x[:, HALF:].astype(BF16).astype(F32), jnp.uint32)
    return (lo >> 16) | (hi & jnp.uint32(0xFFFF0000))


def _unpack_halves(p):
    lo = pltpu.bitcast(p << 16, F32)
    hi = pltpu.bitcast(p & jnp.uint32(0xFFFF0000), F32)
    return lo, hi


def _proj_kernel(x_ref, win_ref, cw_ref, cb_ref, wr_ref, br_ref, wi_ref, bi_ref, lam_ref,
                 qg_ref, wq_ref, kvg_ref, wkv_ref, og_ref, cos_ref, sin_ref,
                 mixl_ref, q_ref, k_ref, v_ref,
                 ubuf, abuf, hbuf, hcar):
    tm = x_ref.shape[0]

    @pl.when(pl.program_id(1) == 0)
    def _():
        ubuf[0:SUBLANES, :] = jnp.zeros((SUBLANES, LRU_WIDTH), F32)
        hcar[...] = jnp.zeros_like(hcar)

    z = jnp.dot(x_ref[...].astype(BF16), win_ref[...], preferred_element_type=F32)
    o1, o2, o3, o4, o5 = LRU_WIDTH, 2 * LRU_WIDTH, 2 * LRU_WIDTH + Q_LORA, IN_COLS - 2 * ROPE_PAIR, IN_COLS - ROPE_PAIR
    u = z[:, :o1]
    gate_in = z[:, o1:o2]

    ubuf[SUBLANES:SUBLANES + tm, :] = u
    cw = cw_ref[...]
    xc = cw[3:4, :] * u + cb_ref[...]
    for j in range(1, CONV_WIDTH):
        xc = xc + cw[3 - j:4 - j, :] * ubuf[SUBLANES - j:SUBLANES - j + tm, :]
    ubuf[0:SUBLANES, :] = ubuf[tm:tm + SUBLANES, :]

    xcb = xc.astype(BF16)
    r = _sigmoid(jnp.dot(xcb, wr_ref[...], preferred_element_type=F32) + br_ref[...])
    ig = _sigmoid(jnp.dot(xcb, wi_ref[...], preferred_element_type=F32) + bi_ref[...])
    nl = -lam_ref[...]
    softplus = jnp.maximum(nl, 0.0) + jnp.log1p(jnp.exp(-jnp.abs(nl)))
    log_a = (-LRU_C * r) * softplus
    a = jnp.exp(log_a)
    one_minus_a2 = -jnp.tanh(log_a) * (a * a + 1.0)
    abuf[...] = a
    hbuf[...] = jnp.sqrt(one_minus_a2) * (ig * xc)

    def scan_row(t, h):
        h = abuf[pl.ds(t, 1), :] * h + hbuf[pl.ds(t, 1), :]
        hbuf[pl.ds(t, 1), :] = h
        return h

    hcar[...] = lax.fori_loop(0, tm, scan_row, hcar[...], unroll=8)

    gelu = 0.5 * gate_in * (1.0 + jnp.tanh(0.7978845608028654 * (gate_in + 0.044715 * (gate_in * gate_in * gate_in))))
    mixl_ref[...] = _rms(hbuf[...] * gelu, og_ref[...]).astype(BF16)

    cos = cos_ref[...]
    sin = sin_ref[...]
    qn = _rms(z[:, o2:o3], qg_ref[...]).astype(BF16)
    qz = jnp.dot(qn, wq_ref[...], preferred_element_type=F32)
    nq = MLA_HEADS * QK_NOPE
    nr = MLA_HEADS * QK_ROPE
    scale = QK_HEAD ** -0.5
    q_ref[:, :nq] = (qz[:, :nq] * scale).astype(BF16)
    q_ref[:, nq:] = ((qz[:, nq:nq + nr] * cos + qz[:, nq + nr:] * sin) * scale).astype(BF16)

    kvn = _rms(z[:, o3:o4], kvg_ref[...]).astype(BF16)
    kvz = jnp.dot(kvn, wkv_ref[...], preferred_element_type=F32)
    k_ref[:, :nq] = kvz[:, :nq].astype(BF16)
    v_ref[...] = kvz[:, nq:].astype(BF16)
    kpe = z[:, o4:o5] * cos[:, :ROPE_PAIR] + z[:, o5:] * sin[:, :ROPE_PAIR]
    lane = lax.broadcasted_iota(jnp.int32, kpe.shape, 1)
    k_ref[:, nq:nq + ROPE_PAIR] = jnp.where(lane < QK_ROPE, kpe, 0.0).astype(BF16)
    k_ref[:, nq + ROPE_PAIR:] = jnp.where(lane >= QK_ROPE, kpe, 0.0).astype(BF16)


def _proj(x, w, cos_t, sin_t):
    bsz, seq, d = x.shape
    nt = seq // TM
    full = lambda a: pl.BlockSpec(a.shape, lambda b, i: (0,) * a.ndim)
    tok = lambda width: pl.BlockSpec((None, TM, width), lambda b, i: (b, i, 0))
    pos = pl.BlockSpec((TM, MLA_HEADS * QK_ROPE), lambda b, i: (i, 0))
    weights = (w["w_in"], w["conv_w"], w["conv_b"], w["w_r"], w["b_r"], w["w_i"], w["b_i"], w["lam"],
               w["q_g"], w["w_q"], w["kv_g"], w["w_kv"], w["og_lru"])
    return pl.pallas_call(
        _proj_kernel,
        grid=(bsz, nt),
        in_specs=[tok(d)] + [full(a) for a in weights] + [pos, pos],
        out_specs=[tok(LRU_WIDTH), tok(Q_COLS), tok(K_COLS), tok(MLA_WIDTH)],
        out_shape=[jax.ShapeDtypeStruct((bsz, seq, LRU_WIDTH), BF16),
                   jax.ShapeDtypeStruct((bsz, seq, Q_COLS), BF16),
                   jax.ShapeDtypeStruct((bsz, seq, K_COLS), BF16),
                   jax.ShapeDtypeStruct((bsz, seq, MLA_WIDTH), BF16)],
        scratch_shapes=[pltpu.VMEM((TM + SUBLANES, LRU_WIDTH), F32),
                        pltpu.VMEM((TM, LRU_WIDTH), F32),
                        pltpu.VMEM((TM, LRU_WIDTH), F32),
                        pltpu.VMEM((1, LRU_WIDTH), F32)],
        compiler_params=pltpu.CompilerParams(dimension_semantics=("arbitrary", "arbitrary"),
                                             vmem_limit_bytes=VMEM_LIMIT),
        name="proj",
    )(x, *weights, cos_t, sin_t)


def _attn_kernel(q_ref, k_ref, v_ref, og_ref, o_ref, m_sc, l_sc, acc_sc):
    i = pl.program_id(1)
    j = pl.program_id(2)
    tq, tk = q_ref.shape[0], k_ref.shape[0]
    nq = MLA_HEADS * QK_NOPE

    @pl.when(j == 0)
    def _():
        m_sc[...] = jnp.full_like(m_sc, NEG_BIG)
        l_sc[...] = jnp.zeros_like(l_sc)
        acc_sc[...] = jnp.zeros_like(acc_sc)

    def step(masked):
        if masked:
            qpos = i * tq + lax.broadcasted_iota(jnp.int32, (tq, tk), 0)
            kpos = j * tk + lax.broadcasted_iota(jnp.int32, (tq, tk), 1)
            keep = kpos <= qpos
        for h in range(MLA_HEADS):
            pj, par = h // 2, h % 2
            qh = jnp.concatenate([q_ref[:, h * QK_NOPE:(h + 1) * QK_NOPE],
                                  q_ref[:, nq + pj * ROPE_PAIR:nq + (pj + 1) * ROPE_PAIR]], axis=-1)
            kh = jnp.concatenate([k_ref[:, h * QK_NOPE:(h + 1) * QK_NOPE],
                                  k_ref[:, nq + par * ROPE_PAIR:nq + (par + 1) * ROPE_PAIR]], axis=-1)
            s = lax.dot_general(qh, kh, (((1,), (1,)), ((), ())), preferred_element_type=F32)
            if masked:
                s = jnp.where(keep, s, NEG_BIG)
            m_prev = m_sc[h]
            m_new = jnp.maximum(m_prev, jnp.max(s, axis=-1, keepdims=True))
            alpha = jnp.exp(m_prev - m_new)
            p = jnp.exp(s - m_new[:, :1])
            l_sc[h] = alpha * l_sc[h] + jnp.sum(p, axis=-1, keepdims=True)
            acc_sc[h] = alpha * acc_sc[h] + jnp.dot(p.astype(BF16), v_ref[:, h * V_HEAD:(h + 1) * V_HEAD],
                                                    preferred_element_type=F32)
            m_sc[h] = m_new

    @pl.when(j < i)
    def _():
        step(False)

    @pl.when(j == i)
    def _():
        step(True)
        out = jnp.concatenate([acc_sc[h] / l_sc[h] for h in range(MLA_HEADS)], axis=-1)
        o_ref[...] = _rms(out, og_ref[...]).astype(BF16)


def _attention(q, k, v, og_mla):
    bsz, seq, _ = q.shape
    nt = seq // TQ
    return pl.pallas_call(
        _attn_kernel,
        grid=(bsz, nt, nt),
        in_specs=[pl.BlockSpec((None, TQ, Q_COLS), lambda b, i, j: (b, i, 0)),
                  pl.BlockSpec((None, TK, K_COLS), lambda b, i, j: (b, jnp.minimum(j, i), 0)),
                  pl.BlockSpec((None, TK, MLA_WIDTH), lambda b, i, j: (b, jnp.minimum(j, i), 0)),
                  pl.BlockSpec(og_mla.shape, lambda b, i, j: (0, 0))],
        out_specs=pl.BlockSpec((None, TQ, MLA_WIDTH), lambda b, i, j: (b, i, 0)),
        out_shape=jax.ShapeDtypeStruct((bsz, seq, MLA_WIDTH), BF16),
        scratch_shapes=[pltpu.VMEM((MLA_HEADS, TQ, LANES), F32),
                        pltpu.VMEM((MLA_HEADS, TQ, LANES), F32),
                        pltpu.VMEM((MLA_HEADS, TQ, V_HEAD), F32)],
        compiler_params=pltpu.CompilerParams(dimension_semantics=("arbitrary", "arbitrary", "arbitrary"),
                                             vmem_limit_bytes=VMEM_LIMIT),
        name="attn",
    )(q, k, v, og_mla)


def _first_max(v, lane):
    m = jnp.max(v, axis=-1, keepdims=True)
    idx = jnp.min(jnp.where(v == m, lane, float(LANES)), axis=-1, keepdims=True)
    return m, idx


def _outproj_kernel(alpha, x_ref, mixl_ref, mixm_ref, wo_ref, g_ref, b_ref, rw_ref, rb_ref,
                    x1_ref, xp_ref, route_ref, cnt_ref):
    h = jnp.dot(mixl_ref[...], wo_ref[:LRU_WIDTH, :], preferred_element_type=F32)
    h = h + jnp.dot(mixm_ref[...], wo_ref[LRU_WIDTH:, :], preferred_element_type=F32)
    x1 = _layer_norm(alpha * x_ref[...] + h, g_ref[...], b_ref[...])
    x1_ref[...] = x1
    xp_ref[...] = _pack_halves(x1)

    s = _sigmoid(jnp.dot(x1.astype(BF16), rw_ref[...], preferred_element_type=F32))
    lane_i = lax.broadcasted_iota(jnp.int32, s.shape, 1)
    lane = lane_i.astype(F32)
    neg = -jnp.inf
    ssel = jnp.where(lane_i < N_EXPERTS, s + rb_ref[...], neg)
    best_g = None
    best_v = None
    for g in range(N_GROUPS):
        in_g = (lane_i >= g * EXPERTS_PER_GROUP) & (lane_i < (g + 1) * EXPERTS_PER_GROUP)
        vg = jnp.where(in_g, ssel, neg)
        m1, i1 = _first_max(vg, lane)
        m2 = jnp.max(jnp.where(lane == i1, neg, vg), axis=-1, keepdims=True)
        score = m1 + m2
        if g == 0:
            best_g, best_v = jnp.zeros_like(m1), score
        else:
            upd = score > best_v
            best_g = jnp.where(upd, float(g), best_g)
            best_v = jnp.where(upd, score, best_v)
    lo_lane = best_g * float(EXPERTS_PER_GROUP)
    vb = jnp.where((lane >= lo_lane) & (lane < lo_lane + float(EXPERTS_PER_GROUP)), ssel, neg)
    _, e0 = _first_max(vb, lane)
    _, e1 = _first_max(jnp.where(lane == e0, neg, vb), lane)
    oh0 = lane == e0
    oh1 = lane == e1
    w0 = jnp.sum(jnp.where(oh0, s, 0.0), axis=-1, keepdims=True)
    w1 = jnp.sum(jnp.where(oh1, s, 0.0), axis=-1, keepdims=True)
    den = w0 + w1
    route = jnp.where(lane_i == 0, e0,
                      jnp.where(lane_i == 1, e1,
                                jnp.where(lane_i == 2, w0 / den, jnp.where(lane_i == 3, w1 / den, 0.0))))
    route_ref[...] = route
    both = oh0.astype(F32) + oh1.astype(F32)
    cnt_ref[...] = jnp.broadcast_to(jnp.sum(both, axis=0, keepdims=True), cnt_ref.shape)


def _outproj(x2, mixl, mixm, w, alpha):
    n, d = x2.shape
    nt = n // TM
    tok = lambda width: pl.BlockSpec((TM, width), lambda i: (i, 0))
    full = lambda a: pl.BlockSpec(a.shape, lambda i: (0,) * a.ndim)
    weights = (w["w_out"], w["ln1_g"], w["ln1_b"], w["router_w"], w["router_b"])
    return pl.pallas_call(
        functools.partial(_outproj_kernel, alpha),
        grid=(nt,),
        in_specs=[tok(d), tok(LRU_WIDTH), tok(MLA_WIDTH)] + [full(a) for a in weights],
        out_specs=[tok(d), tok(HALF), tok(LANES), pl.BlockSpec((SUBLANES, LANES), lambda i: (i, 0))],
        out_shape=[jax.ShapeDtypeStruct((n, d), F32),
                   jax.ShapeDtypeStruct((n, HALF), jnp.uint32),
                   jax.ShapeDtypeStruct((n, LANES), F32),
                   jax.ShapeDtypeStruct((nt * SUBLANES, LANES), F32)],
        compiler_params=pltpu.CompilerParams(dimension_semantics=("arbitrary",), vmem_limit_bytes=VMEM_LIMIT),
        name="outproj",
    )(x2, mixl, mixm, *weights)


def _rank_kernel(route_ref, start_ref, dest_ref, run_sc):
    tm = route_ref.shape[0]

    @pl.when(pl.program_id(0) == 0)
    def _():
        run_sc[...] = start_ref[...]

    route = route_ref[...]
    lane = lax.broadcasted_iota(jnp.int32, route.shape, 1)
    lane_f = lane.astype(F32)
    oh0 = lane_f == jnp.sum(jnp.where(lane == 0, route, 0.0), axis=-1, keepdims=True)
    oh1 = lane_f == jnp.sum(jnp.where(lane == 1, route, 0.0), axis=-1, keepdims=True)
    both = oh0.astype(F32) + oh1.astype(F32)
    row = lax.broadcasted_iota(jnp.int32, (tm, tm), 0)
    col = lax.broadcasted_iota(jnp.int32, (tm, tm), 1)
    strict_lower = (col < row).astype(BF16)
    before = jnp.dot(strict_lower, both.astype(BF16), preferred_element_type=F32) + run_sc[0:1, :]
    d0 = jnp.sum(jnp.where(oh0, before, 0.0), axis=-1, keepdims=True)
    d1 = jnp.sum(jnp.where(oh1, before, 0.0), axis=-1, keepdims=True)
    cols = jnp.where(lane == 0, d0, jnp.where(lane == 1, d1, 0.0))
    dest_ref[...] = cols.T[:SUBLANES, :].astype(jnp.int32)
    run_sc[...] = run_sc[...] + jnp.sum(both, axis=0, keepdims=True)


def _rank(route, start):
    n = route.shape[0]
    nt = n // TM
    return pl.pallas_call(
        _rank_kernel,
        grid=(nt,),
        in_specs=[pl.BlockSpec((TM, LANES), lambda i: (i, 0)),
                  pl.BlockSpec((SUBLANES, LANES), lambda i: (0, 0))],
        out_specs=pl.BlockSpec((SUBLANES, TM), lambda i: (i, 0)),
        out_shape=jax.ShapeDtypeStruct((nt * SUBLANES, TM), jnp.int32),
        scratch_shapes=[pltpu.VMEM((SUBLANES, LANES), F32)],
        compiler_params=pltpu.CompilerParams(dimension_semantics=("arbitrary",)),
        name="rank",
    )(route, start)


def _row_copy(src, src_row, dst, dst_row, sem):
    return pltpu.make_async_copy(src.at[pl.ds(src_row, 1)], dst.at[pl.ds(dst_row, 1)], sem)


def _scatter_kernel(dest_ref, xp_ref, buf_in_ref, buf_ref, sem):
    del buf_in_ref
    tm = xp_ref.shape[0]

    def start(t, c):
        for k in range(TOP_K):
            _row_copy(xp_ref, t, buf_ref, dest_ref[k, t], sem).start()
        return c

    lax.fori_loop(0, tm, start, 0)

    def wait(t, c):
        for k in range(TOP_K):
            _row_copy(xp_ref, 0, buf_ref, 0, sem).wait()
        return c

    lax.fori_loop(0, tm, wait, 0)


def _scatter(dest, xp, buf):
    n = xp.shape[0]
    nt = n // TM
    return pl.pallas_call(
        _scatter_kernel,
        grid=(nt,),
        in_specs=[pl.BlockSpec((SUBLANES, TM), lambda i: (i, 0), memory_space=pltpu.SMEM),
                  pl.BlockSpec((TM, HALF), lambda i: (i, 0)),
                  pl.BlockSpec(memory_space=pl.ANY)],
        out_specs=pl.BlockSpec(memory_space=pl.ANY),
        out_shape=jax.ShapeDtypeStruct(buf.shape, buf.dtype),
        scratch_shapes=[pltpu.SemaphoreType.DMA(())],
        input_output_aliases={2: 0},
        compiler_params=pltpu.CompilerParams(dimension_semantics=("arbitrary",), has_side_effects=True),
        name="scatter",
    )(dest, xp, buf)


def _expert_kernel(te_ref, nv_ref, xs_ref, wg_ref, wu_ref, wd_ref, y_ref, wgu_sc, wd_sc):
    i = pl.program_id(0)
    e = te_ref[i]
    prev = te_ref[jnp.maximum(i - 1, 0)]

    @pl.when((i == 0) | (e != prev))
    def _():
        wgu_sc[:, :D_EXPERT] = wg_ref[...].astype(BF16)
        wgu_sc[:, D_EXPERT:] = wu_ref[...].astype(BF16)
        wd_sc[...] = wd_ref[...].astype(BF16)

    @pl.when(i < nv_ref[0])
    def _():
        lo, hi = _unpack_halves(xs_ref[...])
        gu = jnp.dot(lo.astype(BF16), wgu_sc[:HALF, :], preferred_element_type=F32)
        gu = gu + jnp.dot(hi.astype(BF16), wgu_sc[HALF:, :], preferred_element_type=F32)
        g = gu[:, :D_EXPERT]
        hb = (g * _sigmoid(g)) * gu[:, D_EXPERT:]
        y = jnp.dot(hb.astype(BF16), wd_sc[...], preferred_element_type=F32)
        y_ref[...] = _pack_halves(y)

    @pl.when(i >= nv_ref[0])
    def _():
        y_ref[...] = jnp.zeros_like(y_ref)


def _experts(layer, tile_expert, n_valid, xs, w_gate, w_up, w_down):
    rows = xs.shape[0]
    nt = rows // TE
    wspec = lambda shape: pl.BlockSpec((None, None) + shape, lambda i, te, nv: (layer, te[i], 0, 0))
    grid_spec = pltpu.PrefetchScalarGridSpec(
        num_scalar_prefetch=2,
        grid=(nt,),
        in_specs=[pl.BlockSpec((TE, HALF), lambda i, te, nv: (i, 0)),
                  wspec((D_MODEL, D_EXPERT)), wspec((D_MODEL, D_EXPERT)), wspec((D_EXPERT, D_MODEL))],
        out_specs=pl.BlockSpec((TE, HALF), lambda i, te, nv: (i, 0)),
        scratch_shapes=[pltpu.VMEM((D_MODEL, 2 * D_EXPERT), BF16),
                        pltpu.VMEM((D_EXPERT, D_MODEL), BF16)],
    )
    return pl.pallas_call(
        _expert_kernel,
        grid_spec=grid_spec,
        out_shape=jax.ShapeDtypeStruct((rows, HALF), jnp.uint32),
        compiler_params=pltpu.CompilerParams(dimension_semantics=("arbitrary",), vmem_limit_bytes=VMEM_LIMIT),
        name="experts",
    )(tile_expert, n_valid, xs, w_gate, w_up, w_down)


def _combine_kernel(alpha, dest_ref, route_ref, x1_ref, g_ref, b_ref, ys_ref, o_ref, ybuf, sem):
    tm = x1_ref.shape[0]

    def start(t, c):
        for k in range(TOP_K):
            _row_copy(ys_ref, dest_ref[k, t], ybuf.at[k], t, sem).start()
        return c

    lax.fori_loop(0, tm, start, 0)

    def wait(t, c):
        for k in range(TOP_K):
            _row_copy(ys_ref, 0, ybuf.at[k], 0, sem).wait()
        return c

    lax.fori_loop(0, tm, wait, 0)

    route = route_ref[...]
    lane = lax.broadcasted_iota(jnp.int32, route.shape, 1)
    g0 = jnp.sum(jnp.where(lane == 2, route, 0.0), axis=-1, keepdims=True)
    g1 = jnp.sum(jnp.where(lane == 3, route, 0.0), axis=-1, keepdims=True)
    lo0, hi0 = _unpack_halves(ybuf[0])
    lo1, hi1 = _unpack_halves(ybuf[1])
    f = jnp.concatenate([lo0 * g0 + lo1 * g1, hi0 * g0 + hi1 * g1], axis=-1)
    o_ref[...] = _layer_norm(alpha * x1_ref[...] + f, g_ref[...], b_ref[...])


def _combine(dest, route, x1, ln_g, ln_b, ys, alpha):
    n, d = x1.shape
    nt = n // TM
    return pl.pallas_call(
        functools.partial(_combine_kernel, alpha),
        grid=(nt,),
        in_specs=[pl.BlockSpec((SUBLANES, TM), lambda i: (i, 0), memory_space=pltpu.SMEM),
                  pl.BlockSpec((TM, LANES), lambda i: (i, 0)),
                  pl.BlockSpec((TM, d), lambda i: (i, 0)),
                  pl.BlockSpec(ln_g.shape, lambda i: (0, 0)),
                  pl.BlockSpec(ln_b.shape, lambda i: (0, 0)),
                  pl.BlockSpec(memory_space=pl.ANY)],
        out_specs=pl.BlockSpec((TM, d), lambda i: (i, 0)),
        out_shape=jax.ShapeDtypeStruct((n, d), F32),
        scratch_shapes=[pltpu.VMEM((TOP_K, TM, HALF), jnp.uint32),
                        pltpu.SemaphoreType.DMA(())],
        compiler_params=pltpu.CompilerParams(dimension_semantics=("arbitrary",), vmem_limit_bytes=VMEM_LIMIT),
        name="combine",
    )(dest, route, x1, ln_g, ln_b, ys)


def _rope_tables(seq):
    half = QK_ROPE // 2
    inv = ROPE_BASE ** (-jnp.arange(half, dtype=F32) / half)
    ang = jnp.arange(seq).astype(F32)[:, None] * inv[None, :]
    cos, sin = jnp.cos(ang), jnp.sin(ang)
    cos_t = jnp.tile(cos, (1, 2 * MLA_HEADS))
    sin_t = jnp.tile(jnp.concatenate([-sin, sin], axis=-1), (1, MLA_HEADS))
    return cos_t, sin_t


def _block_diag(w):
    nb, c, dd = w.shape
    eye = jnp.eye(nb, dtype=w.dtype)
    return (eye[:, None, :, None] * w[:, :, None, :]).reshape(nb * c, nb * dd)


def _layer_weights(l, w_in, conv_w, conv_b, w_rgate, b_rgate, w_igate, b_igate, lru_lambda,
                   q_norm_g, w_q_up, kv_norm_g, w_kv_up, out_norm_g, w_out, ln1_g, ln1_b,
                   router_w, router_bias, ln2_g, ln2_b):
    swap = (jnp.arange(QK_ROPE) + QK_ROPE // 2) % QK_ROPE
    o4 = 2 * LRU_WIDTH + Q_LORA + KV_LORA
    kr = w_in[l][:, o4:]
    w_in_l = jnp.concatenate([w_in[l][:, :o4], kr, kr, kr[:, swap], kr[:, swap]], axis=-1)
    wq = w_q_up[l]
    wq_l = jnp.concatenate([wq[:, :, :QK_NOPE].reshape(Q_LORA, -1),
                            wq[:, :, QK_NOPE:].reshape(Q_LORA, -1),
                            wq[:, :, QK_NOPE:][:, :, swap].reshape(Q_LORA, -1)], axis=-1)
    wkv = w_kv_up[l]
    wkv_l = jnp.concatenate([wkv[:, :, :QK_NOPE].reshape(KV_LORA, -1),
                             wkv[:, :, QK_NOPE:].reshape(KV_LORA, -1)], axis=-1)
    row = lambda a: a.reshape(1, -1)
    pad = LANES - N_EXPERTS
    return {
        "w_in": w_in_l.astype(BF16), "conv_w": conv_w[l], "conv_b": row(conv_b[l]),
        "w_r": _block_diag(w_rgate[l]).astype(BF16), "b_r": row(b_rgate[l]),
        "w_i": _block_diag(w_igate[l]).astype(BF16), "b_i": row(b_igate[l]),
        "lam": row(lru_lambda[l]), "q_g": row(q_norm_g[l]), "w_q": wq_l.astype(BF16),
        "kv_g": row(kv_norm_g[l]), "w_kv": wkv_l.astype(BF16),
        "og_lru": row(out_norm_g[l][:LRU_WIDTH]), "og_mla": row(out_norm_g[l][LRU_WIDTH:]),
        "w_out": w_out[l].astype(BF16), "ln1_g": row(ln1_g[l]), "ln1_b": row(ln1_b[l]),
        "router_w": jnp.pad(router_w, ((0, 0), (0, pad))).astype(BF16),
        "router_b": jnp.pad(router_bias, (0, pad)).reshape(1, -1),
        "ln2_g": row(ln2_g[l]), "ln2_b": row(ln2_b[l]),
    }


def _dispatch_plan(counts, n_rows):
    cnt = counts.reshape(-1, SUBLANES, LANES)[:, 0, :N_EXPERTS].sum(axis=0).astype(jnp.int32)
    padded = (cnt + TE - 1) // TE * TE
    pad_end = jnp.cumsum(padded)
    pad_start = pad_end - padded
    start = jnp.zeros((SUBLANES, LANES), F32).at[:, :N_EXPERTS].set(pad_start.astype(F32)[None, :])
    tile_row = jnp.arange(n_rows // TE, dtype=jnp.int32) * TE
    tile_expert = jnp.minimum(jnp.searchsorted(pad_end, tile_row, side="right"), N_EXPERTS - 1).astype(jnp.int32)
    n_valid = (pad_end[-1:] // TE).astype(jnp.int32)
    return start, tile_expert, n_valid


def kernel(x, w_in, conv_w, conv_b, w_rgate, b_rgate, w_igate, b_igate, lru_lambda, q_norm_g, w_q_up, kv_norm_g, w_kv_up, out_norm_g, w_out, ln1_g, ln1_b, router_w, router_bias, w_exp_gate, w_exp_up, w_exp_down, ln2_g, ln2_b):
    bsz, seq, d = x.shape
    depth = w_in.shape[0]
    assert d == D_MODEL and seq % TM == 0 and seq % TQ == 0 and TQ == TK
    alpha = float((2 * depth) ** 0.25)
    n = bsz * seq
    n_rows = n * TOP_K + N_EXPERTS * TE
    cos_t, sin_t = _rope_tables(seq)
    for l in range(depth):
        w = _layer_weights(l, w_in, conv_w, conv_b, w_rgate, b_rgate, w_igate, b_igate, lru_lambda,
                           q_norm_g, w_q_up, kv_norm_g, w_kv_up, out_norm_g, w_out, ln1_g, ln1_b,
                           router_w, router_bias, ln2_g, ln2_b)
        mixl, q, k, v = _proj(x, w, cos_t, sin_t)
        mixm = _attention(q, k, v, w["og_mla"])
        x1, xp, route, counts = _outproj(x.reshape(n, d), mixl.reshape(n, -1), mixm.reshape(n, -1), w, alpha)
        start, tile_expert, n_valid = _dispatch_plan(counts, n_rows)
        dest = _rank(route, start)
        xs = _scatter(dest, xp, jnp.zeros((n_rows, HALF), jnp.uint32))
        ys = _experts(l, tile_expert, n_valid, xs, w_exp_gate, w_exp_up, w_exp_down)
        x = _combine(dest, route, x1, w["ln2_g"], w["ln2_b"], ys, alpha).reshape(bsz, seq, d)
    return x
```

```python
import functools

import jax
import jax.numpy as jnp
from jax import lax
from jax.experimental import pallas as pl
from jax.experimental.pallas import tpu as pltpu

D_MODEL = 1024
LRU_WIDTH = 512
LRU_BLOCKS = 8
CONV_WIDTH = 4
LRU_C = 8.0
MLA_HEADS = 4
QK_NOPE = 128
QK_ROPE = 64
QK_HEAD = QK_NOPE + QK_ROPE
V_HEAD = 128
Q_LORA = 768
KV_LORA = 256
MLA_WIDTH = MLA_HEADS * V_HEAD
ROPE_BASE = 10000.0
N_EXPERTS = 16
N_GROUPS = 4
EXPERTS_PER_GROUP = N_EXPERTS // N_GROUPS
TOP_K = 2
D_EXPERT = 512
EPS = 1e-6

LANES = 128
SUBLANES = 8
VMEM_LIMIT = 56 * 1024 * 1024

ROPE_PAIR = 2 * QK_ROPE
IN_COLS = 2 * LRU_WIDTH + Q_LORA + KV_LORA + 2 * ROPE_PAIR
Q_COLS = MLA_HEADS * QK_NOPE + MLA_HEADS * QK_ROPE
K_COLS = MLA_HEADS * QK_NOPE + 2 * ROPE_PAIR

TM = 512
TQ = 512
TK = 512
TE = 256

SEG_ALIGN = 2 * SUBLANES
SORTED_ROWS = TOP_K * TM + N_EXPERTS * SEG_ALIGN
SEG_BITS = (TM // SEG_ALIGN).bit_length()
TAIL_BITS = (TE // SEG_ALIGN - 1).bit_length()
TAB_GBASE, TAB_SEGOFF, TAB_SEGROWS, TAB_TAIL_START, TAB_TAIL_N = (k * N_EXPERTS for k in range(5))
TAB_SLACK_START = 5 * N_EXPERTS
ZERO_ROWS = TE // 2

F32 = jnp.float32
BF16 = jnp.bfloat16
NEG_BIG = -1e30


def _rms(x, g):
    ms = jnp.mean(x * x, axis=-1, keepdims=True)
    return x * lax.rsqrt(ms + EPS) * g


def _layer_norm(x, g, b):
    mu = jnp.mean(x, axis=-1, keepdims=True)
    xc = x - mu
    var = jnp.mean(xc * xc, axis=-1, keepdims=True)
    return xc * lax.rsqrt(var + EPS) * g + b


def _sigmoid(x):
    return 1.0 / (1.0 + jnp.exp(-x))


def _proj_kernel(x_ref, win_ref, cw_ref, cb_ref, wr_ref, br_ref, wi_ref, bi_ref, lam_ref,
                 qg_ref, wq_ref, kvg_ref, wkv_ref, og_ref, cos_ref, sin_ref,
                 mixl_ref, q_ref, k_ref, v_ref,
                 ubuf, abuf, hbuf, hcar):
    tm = x_ref.shape[0]

    @pl.when(pl.program_id(1) == 0)
    def _():
        ubuf[0:SUBLANES, :] = jnp.zeros((SUBLANES, LRU_WIDTH), F32)
        hcar[...] = jnp.zeros_like(hcar)

    z = jnp.dot(x_ref[...].astype(BF16), win_ref[...], preferred_element_type=F32)
    o1, o2, o3, o4, o5 = LRU_WIDTH, 2 * LRU_WIDTH, 2 * LRU_WIDTH + Q_LORA, IN_COLS - 2 * ROPE_PAIR, IN_COLS - ROPE_PAIR
    u = z[:, :o1]
    gate_in = z[:, o1:o2]

    ubuf[SUBLANES:SUBLANES + tm, :] = u
    cw = cw_ref[...]
    xc = cw[3:4, :] * u + cb_ref[...]
    for j in range(1, CONV_WIDTH):
        xc = xc + cw[3 - j:4 - j, :] * ubuf[SUBLANES - j:SUBLANES - j + tm, :]
    ubuf[0:SUBLANES, :] = ubuf[tm:tm + SUBLANES, :]

    xcb = xc.astype(BF16)
    r = _sigmoid(jnp.dot(xcb, wr_ref[...], preferred_element_type=F32) + br_ref[...])
    ig = _sigmoid(jnp.dot(xcb, wi_ref[...], preferred_element_type=F32) + bi_ref[...])
    nl = -lam_ref[...]
    softplus = jnp.maximum(nl, 0.0) + jnp.log1p(jnp.exp(-jnp.abs(nl)))
    log_a = (-LRU_C * r) * softplus
    a = jnp.exp(log_a)
    one_minus_a2 = -jnp.tanh(log_a) * (a * a + 1.0)
    abuf[...] = a
    hbuf[...] = jnp.sqrt(one_minus_a2) * (ig * xc)

    def scan_row(t, h):
        h = abuf[pl.ds(t, 1), :] * h + hbuf[pl.ds(t, 1), :]
        hbuf[pl.ds(t, 1), :] = h
        return h

    hcar[...] = lax.fori_loop(0, tm, scan_row, hcar[...], unroll=8)

    gelu = 0.5 * gate_in * (1.0 + jnp.tanh(0.7978845608028654 * (gate_in + 0.044715 * (gate_in * gate_in * gate_in))))
    mixl_ref[...] = _rms(hbuf[...] * gelu, og_ref[...]).astype(BF16)

    cos = cos_ref[...]
    sin = sin_ref[...]
    qn = _rms(z[:, o2:o3], qg_ref[...]).astype(BF16)
    qz = jnp.dot(qn, wq_ref[...], preferred_element_type=F32)
    nq = MLA_HEADS * QK_NOPE
    nr = MLA_HEADS * QK_ROPE
    scale = QK_HEAD ** -0.5
    q_ref[:, :nq] = (qz[:, :nq] * scale).astype(BF16)
    q_ref[:, nq:] = ((qz[:, nq:nq + nr] * cos + qz[:, nq + nr:] * sin) * scale).astype(BF16)

    kvn = _rms(z[:, o3:o4], kvg_ref[...]).astype(BF16)
    kvz = jnp.dot(kvn, wkv_ref[...], preferred_element_type=F32)
    k_ref[:, :nq] = kvz[:, :nq].astype(BF16)
    v_ref[...] = kvz[:, nq:].astype(BF16)
    kpe = z[:, o4:o5] * cos[:, :ROPE_PAIR] + z[:, o5:] * sin[:, :ROPE_PAIR]
    lane = lax.broadcasted_iota(jnp.int32, kpe.shape, 1)
    k_ref[:, nq:nq + ROPE_PAIR] = jnp.where(lane < QK_ROPE, kpe, 0.0).astype(BF16)
    k_ref[:, nq + ROPE_PAIR:] = jnp.where(lane >= QK_ROPE, kpe, 0.0).astype(BF16)


def _proj(x, w, cos_t, sin_t):
    bsz, seq, d = x.shape
    nt = seq // TM
    full = lambda a: pl.BlockSpec(a.shape, lambda b, i: (0,) * a.ndim)
    tok = lambda width: pl.BlockSpec((None, TM, width), lambda b, i: (b, i, 0))
    pos = pl.BlockSpec((TM, MLA_HEADS * QK_ROPE), lambda b, i: (i, 0))
    weights = (w["w_in"], w["conv_w"], w["conv_b"], w["w_r"], w["b_r"], w["w_i"], w["b_i"], w["lam"],
               w["q_g"], w["w_q"], w["kv_g"], w["w_kv"], w["og_lru"])
    return pl.pallas_call(
        _proj_kernel,
        grid=(bsz, nt),
        in_specs=[tok(d)] + [full(a) for a in weights] + [pos, pos],
        out_specs=[tok(LRU_WIDTH), tok(Q_COLS), tok(K_COLS), tok(MLA_WIDTH)],
        out_shape=[jax.ShapeDtypeStruct((bsz, seq, LRU_WIDTH), BF16),
                   jax.ShapeDtypeStruct((bsz, seq, Q_COLS), BF16),
                   jax.ShapeDtypeStruct((bsz, seq, K_COLS), BF16),
                   jax.ShapeDtypeStruct((bsz, seq, MLA_WIDTH), BF16)],
        scratch_shapes=[pltpu.VMEM((TM + SUBLANES, LRU_WIDTH), F32),
                        pltpu.VMEM((TM, LRU_WIDTH), F32),
                        pltpu.VMEM((TM, LRU_WIDTH), F32),
                        pltpu.VMEM((1, LRU_WIDTH), F32)],
        compiler_params=pltpu.CompilerParams(dimension_semantics=("arbitrary", "arbitrary"),
                                             vmem_limit_bytes=VMEM_LIMIT),
        name="proj",
    )(x, *weights, cos_t, sin_t)


def _attn_kernel(q_ref, k_ref, v_ref, og_ref, o_ref, m_sc, l_sc, acc_sc):
    i = pl.program_id(1)
    j = pl.program_id(2)
    tq, tk = q_ref.shape[0], k_ref.shape[0]
    nq = MLA_HEADS * QK_NOPE

    @pl.when(j == 0)
    def _():
        m_sc[...] = jnp.full_like(m_sc, NEG_BIG)
        l_sc[...] = jnp.zeros_like(l_sc)
        acc_sc[...] = jnp.zeros_like(acc_sc)

    def step(masked):
        if masked:
            qpos = i * tq + lax.broadcasted_iota(jnp.int32, (tq, tk), 0)
            kpos = j * tk + lax.broadcasted_iota(jnp.int32, (tq, tk), 1)
            keep = kpos <= qpos
        for h in range(MLA_HEADS):
            pj, par = h // 2, h % 2
            qh = jnp.concatenate([q_ref[:, h * QK_NOPE:(h + 1) * QK_NOPE],
                                  q_ref[:, nq + pj * ROPE_PAIR:nq + (pj + 1) * ROPE_PAIR]], axis=-1)
            kh = jnp.concatenate([k_ref[:, h * QK_NOPE:(h + 1) * QK_NOPE],
                                  k_ref[:, nq + par * ROPE_PAIR:nq + (par + 1) * ROPE_PAIR]], axis=-1)
            s = lax.dot_general(qh, kh, (((1,), (1,)), ((), ())), preferred_element_type=F32)
            if masked:
                s = jnp.where(keep, s, NEG_BIG)
            m_prev = m_sc[h]
            m_new = jnp.maximum(m_prev, jnp.max(s, axis=-1, keepdims=True))
            alpha = jnp.exp(m_prev - m_new)
            p = jnp.exp(s - m_new[:, :1])
            l_sc[h] = alpha * l_sc[h] + jnp.sum(p, axis=-1, keepdims=True)
            acc_sc[h] = alpha * acc_sc[h] + jnp.dot(p.astype(BF16), v_ref[:, h * V_HEAD:(h + 1) * V_HEAD],
                                                    preferred_element_type=F32)
            m_sc[h] = m_new

    @pl.when(j < i)
    def _():
        step(False)

    @pl.when(j == i)
    def _():
        step(True)
        out = jnp.concatenate([acc_sc[h] / l_sc[h] for h in range(MLA_HEADS)], axis=-1)
        o_ref[...] = _rms(out, og_ref[...]).astype(BF16)


def _attention(q, k, v, og_mla):
    bsz, seq, _ = q.shape
    nt = seq // TQ
    return pl.pallas_call(
        _attn_kernel,
        grid=(bsz, nt, nt),
        in_specs=[pl.BlockSpec((None, TQ, Q_COLS), lambda b, i, j: (b, i, 0)),
                  pl.BlockSpec((None, TK, K_COLS), lambda b, i, j: (b, jnp.minimum(j, i), 0)),
                  pl.BlockSpec((None, TK, MLA_WIDTH), lambda b, i, j: (b, jnp.minimum(j, i), 0)),
                  pl.BlockSpec(og_mla.shape, lambda b, i, j: (0, 0))],
        out_specs=pl.BlockSpec((None, TQ, MLA_WIDTH), lambda b, i, j: (b, i, 0)),
        out_shape=jax.ShapeDtypeStruct((bsz, seq, MLA_WIDTH), BF16),
        scratch_shapes=[pltpu.VMEM((MLA_HEADS, TQ, LANES), F32),
                        pltpu.VMEM((MLA_HEADS, TQ, LANES), F32),
                        pltpu.VMEM((MLA_HEADS, TQ, V_HEAD), F32)],
        compiler_params=pltpu.CompilerParams(dimension_semantics=("arbitrary", "arbitrary", "arbitrary"),
                                             vmem_limit_bytes=VMEM_LIMIT),
        name="attn",
    )(q, k, v, og_mla)


def _first_max(v, lane):
    m = jnp.max(v, axis=-1, keepdims=True)
    idx = jnp.min(jnp.where(v == m, lane, float(LANES)), axis=-1, keepdims=True)
    return m, idx


def _outproj_kernel(alpha, x_ref, mixl_ref, mixm_ref, wo_ref, g_ref, b_ref, rw_ref, rb_ref,
                    x1_ref, xb_ref, route_ref, cnt_ref):
    h = jnp.dot(mixl_ref[...], wo_ref[:LRU_WIDTH, :], preferred_element_type=F32)
    h = h + jnp.dot(mixm_ref[...], wo_ref[LRU_WIDTH:, :], preferred_element_type=F32)
    x1 = _layer_norm(alpha * x_ref[...] + h, g_ref[...], b_ref[...])
    x1_ref[...] = x1
    xb = x1.astype(BF16)
    xb_ref[...] = xb

    s = _sigmoid(jnp.dot(xb, rw_ref[...], preferred_element_type=F32))
    lane_i = lax.broadcasted_iota(jnp.int32, s.shape, 1)
    lane = lane_i.astype(F32)
    neg = -jnp.inf
    ssel = jnp.where(lane_i < N_EXPERTS, s + rb_ref[...], neg)
    best_g = None
    best_v = None
    for g in range(N_GROUPS):
        in_g = (lane_i >= g * EXPERTS_PER_GROUP) & (lane_i < (g + 1) * EXPERTS_PER_GROUP)
        vg = jnp.where(in_g, ssel, neg)
        m1, i1 = _first_max(vg, lane)
        m2 = jnp.max(jnp.where(lane == i1, neg, vg), axis=-1, keepdims=True)
        score = m1 + m2
        if g == 0:
            best_g, best_v = jnp.zeros_like(m1), score
        else:
            upd = score > best_v
            best_g = jnp.where(upd, float(g), best_g)
            best_v = jnp.where(upd, score, best_v)
    lo_lane = best_g * float(EXPERTS_PER_GROUP)
    vb = jnp.where((lane >= lo_lane) & (lane < lo_lane + float(EXPERTS_PER_GROUP)), ssel, neg)
    _, e0 = _first_max(vb, lane)
    _, e1 = _first_max(jnp.where(lane == e0, neg, vb), lane)
    oh0 = lane == e0
    oh1 = lane == e1
    w0 = jnp.sum(jnp.where(oh0, s, 0.0), axis=-1, keepdims=True)
    w1 = jnp.sum(jnp.where(oh1, s, 0.0), axis=-1, keepdims=True)
    den = w0 + w1
    route = jnp.where(lane_i == 0, e0,
                      jnp.where(lane_i == 1, e1,
                                jnp.where(lane_i == 2, w0 / den, jnp.where(lane_i == 3, w1 / den, 0.0))))
    route_ref[...] = route
    both = oh0.astype(F32) + oh1.astype(F32)
    cnt_ref[...] = jnp.broadcast_to(jnp.sum(both, axis=0, keepdims=True), cnt_ref.shape)


def _outproj(x2, mixl, mixm, w, alpha):
    n, d = x2.shape
    nt = n // TM
    tok = lambda width: pl.BlockSpec((TM, width), lambda i: (i, 0))
    full = lambda a: pl.BlockSpec(a.shape, lambda i: (0,) * a.ndim)
    weights = (w["w_out"], w["ln1_g"], w["ln1_b"], w["router_w"], w["router_b"])
    return pl.pallas_call(
        functools.partial(_outproj_kernel, alpha),
        grid=(nt,),
        in_specs=[tok(d), tok(LRU_WIDTH), tok(MLA_WIDTH)] + [full(a) for a in weights],
        out_specs=[tok(d), tok(d), tok(LANES), pl.BlockSpec((SUBLANES, LANES), lambda i: (i, 0))],
        out_shape=[jax.ShapeDtypeStruct((n, d), F32),
                   jax.ShapeDtypeStruct((n, d), BF16),
                   jax.ShapeDtypeStruct((n, LANES), F32),
                   jax.ShapeDtypeStruct((nt * SUBLANES, LANES), F32)],
        compiler_params=pltpu.CompilerParams(dimension_semantics=("arbitrary",), vmem_limit_bytes=VMEM_LIMIT),
        name="outproj",
    )(x2, mixl, mixm, *weights)


def _lane_column(a, lane, k):
    return jnp.sum(jnp.where(lane == k, a, 0.0), axis=-1, keepdims=True)


def _for_each_piece(n, bits, fn):
    pos = 0
    for b in reversed(range(bits)):
        size = SEG_ALIGN << b
        take = (n & size) != 0

        @pl.when(take)
        def _(pos=pos, size=size):
            fn(pos, size)

        pos = pos + jnp.where(take, size, 0)


def _segment_copy(src, src_row, dst, dst_row, size, sem):
    aligned = lambda r: r if isinstance(r, int) else pl.multiple_of(r, SEG_ALIGN)
    return pltpu.make_async_copy(src.at[pl.ds(aligned(src_row), size)], dst.at[pl.ds(aligned(dst_row), size)], sem)


def _scatter_kernel(tab_ref, route_ref, segoff_ref, xb_ref, xs_ref, gs_ref, loc_ref,
                    sx_sc, sg_sc, zx_sc, zg_sc, sems):
    i = pl.program_id(0)
    tm = route_ref.shape[0]
    route = route_ref[...]
    lane = lax.broadcasted_iota(jnp.int32, route.shape, 1)
    lane_f = lane.astype(F32)
    oh0 = lane_f == _lane_column(route, lane, 0)
    oh1 = lane_f == _lane_column(route, lane, 1)
    both = oh0.astype(F32) + oh1.astype(F32)
    row = lax.broadcasted_iota(jnp.int32, (tm, tm), 0)
    col = lax.broadcasted_iota(jnp.int32, (tm, tm), 1)
    strict_lower = (col < row).astype(BF16)
    before = jnp.dot(strict_lower, both.astype(BF16), preferred_element_type=F32) + segoff_ref[0:1, :]
    loc0 = jnp.sum(jnp.where(oh0, before, 0.0), axis=-1, keepdims=True)
    loc1 = jnp.sum(jnp.where(oh1, before, 0.0), axis=-1, keepdims=True)
    loc = jnp.where(lane == 0, loc0, jnp.where(lane == 1, loc1, jnp.where((lane == 2) | (lane == 3), route, 0.0)))
    loc_ref[...] = loc
    loc_t = loc.T
    srow = lax.broadcasted_iota(jnp.int32, (SORTED_ROWS, tm), 0).astype(F32)
    p0 = srow == loc_t[0:1, :]
    p1 = srow == loc_t[1:2, :]
    perm = jnp.where(p0 | p1, 1.0, 0.0).astype(BF16)
    sx_sc[...] = jnp.dot(perm, xb_ref[...], preferred_element_type=F32).astype(BF16)
    gate = jnp.sum(jnp.where(p0, loc_t[2:3, :], 0.0) + jnp.where(p1, loc_t[3:4, :], 0.0), axis=-1, keepdims=True)
    sg_sc[...] = jnp.broadcast_to(gate, sg_sc.shape)

    def both_copies(op, src_x, src_g, src_row, dst_row, size):
        op(_segment_copy(src_x, src_row, xs_ref, dst_row, size, sems.at[0]))
        op(_segment_copy(src_g, src_row, gs_ref, dst_row, size, sems.at[1]))

    def segments(op):
        for e in range(N_EXPERTS):
            gbase = tab_ref[i, TAB_GBASE + e]
            off = tab_ref[i, TAB_SEGOFF + e]
            _for_each_piece(tab_ref[i, TAB_SEGROWS + e], SEG_BITS,
                            lambda pos, size: both_copies(op, sx_sc, sg_sc, off + pos, gbase + pos, size))

    def tails(op):
        for e in range(N_EXPERTS):
            start = tab_ref[i, TAB_TAIL_START + e]
            _for_each_piece(tab_ref[i, TAB_TAIL_N + e], TAIL_BITS,
                            lambda pos, size: both_copies(op, zx_sc, zg_sc, 0, start + pos, size))

    segments(lambda c: c.start())

    @pl.when(i == pl.num_programs(0) - 1)
    def _():
        zx_sc[...] = jnp.zeros_like(zx_sc)
        zg_sc[...] = jnp.zeros_like(zg_sc)
        slack_start = tab_ref[i, TAB_SLACK_START]
        n_slack = (xs_ref.shape[0] - slack_start) // ZERO_ROWS

        def slack(op):
            def body(j, c):
                both_copies(op, zx_sc, zg_sc, 0, slack_start + j * ZERO_ROWS, ZERO_ROWS)
                return c
            lax.fori_loop(0, n_slack, body, 0)

        tails(lambda c: c.start())
        slack(lambda c: c.start())
        tails(lambda c: c.wait())
        slack(lambda c: c.wait())

    segments(lambda c: c.wait())


def _scatter(tab, route, segoff, xb, n_rows):
    n = xb.shape[0]
    nt = n // TM
    grid_spec = pltpu.PrefetchScalarGridSpec(
        num_scalar_prefetch=1,
        grid=(nt,),
        in_specs=[pl.BlockSpec((TM, LANES), lambda i, tab: (i, 0)),
                  pl.BlockSpec((SUBLANES, LANES), lambda i, tab: (i, 0)),
                  pl.BlockSpec((TM, D_MODEL), lambda i, tab: (i, 0))],
        out_specs=[pl.BlockSpec(memory_space=pl.ANY),
                   pl.BlockSpec(memory_space=pl.ANY),
                   pl.BlockSpec((TM, LANES), lambda i, tab: (i, 0))],
        scratch_shapes=[pltpu.VMEM((SORTED_ROWS, D_MODEL), BF16),
                        pltpu.VMEM((SORTED_ROWS, LANES), F32),
                        pltpu.VMEM((ZERO_ROWS, D_MODEL), BF16),
                        pltpu.VMEM((ZERO_ROWS, LANES), F32),
                        pltpu.SemaphoreType.DMA((2,))],
    )
    return pl.pallas_call(
        _scatter_kernel,
        grid_spec=grid_spec,
        out_shape=[jax.ShapeDtypeStruct((n_rows, D_MODEL), BF16),
                   jax.ShapeDtypeStruct((n_rows, LANES), F32),
                   jax.ShapeDtypeStruct((n, LANES), F32)],
        compiler_params=pltpu.CompilerParams(dimension_semantics=("arbitrary",), vmem_limit_bytes=VMEM_LIMIT),
        name="scatter",
    )(tab, route, segoff, xb)


def _expert_kernel(te_ref, nv_ref, xs_ref, gs_ref, wg_ref, wu_ref, wd_ref, y_ref, wgu_sc, wd_sc):
    i = pl.program_id(0)
    e = te_ref[i]
    prev = te_ref[jnp.maximum(i - 1, 0)]

    @pl.when((i == 0) | (e != prev))
    def _():
        wgu_sc[:, :D_EXPERT] = wg_ref[...].astype(BF16)
        wgu_sc[:, D_EXPERT:] = wu_ref[...].astype(BF16)
        wd_sc[...] = wd_ref[...].astype(BF16)

    @pl.when(i < nv_ref[0])
    def _():
        gu = jnp.dot(xs_ref[...], wgu_sc[...], preferred_element_type=F32)
        g = gu[:, :D_EXPERT]
        hb = (g * _sigmoid(g)) * gu[:, D_EXPERT:]
        y = jnp.dot(hb.astype(BF16), wd_sc[...], preferred_element_type=F32)
        y_ref[...] = (y * gs_ref[:, :1]).astype(BF16)

    @pl.when(i >= nv_ref[0])
    def _():
        y_ref[...] = jnp.zeros_like(y_ref)


def _experts(layer, tile_expert, n_valid, xs, gs, w_gate, w_up, w_down):
    rows = xs.shape[0]
    nt = rows // TE
    wspec = lambda shape: pl.BlockSpec((None, None) + shape, lambda i, te, nv: (layer, te[i], 0, 0))
    rowspec = lambda width: pl.BlockSpec((TE, width), lambda i, te, nv: (jnp.minimum(i, nv[0] - 1), 0))
    grid_spec = pltpu.PrefetchScalarGridSpec(
        num_scalar_prefetch=2,
        grid=(nt,),
        in_specs=[rowspec(D_MODEL), rowspec(LANES),
                  wspec((D_MODEL, D_EXPERT)), wspec((D_MODEL, D_EXPERT)), wspec((D_EXPERT, D_MODEL))],
        out_specs=pl.BlockSpec((TE, D_MODEL), lambda i, te, nv: (i, 0)),
        scratch_shapes=[pltpu.VMEM((D_MODEL, 2 * D_EXPERT), BF16),
                        pltpu.VMEM((D_EXPERT, D_MODEL), BF16)],
    )
    return pl.pallas_call(
        _expert_kernel,
        grid_spec=grid_spec,
        out_shape=jax.ShapeDtypeStruct((rows, D_MODEL), BF16),
        compiler_params=pltpu.CompilerParams(dimension_semantics=("arbitrary",), vmem_limit_bytes=VMEM_LIMIT),
        name="experts",
    )(tile_expert, n_valid, xs, gs, w_gate, w_up, w_down)


def _combine_kernel(alpha, tab_ref, loc_ref, x1_ref, g_ref, b_ref, ys_ref, o_ref, ybuf, sem):
    i = pl.program_id(0)
    tm = x1_ref.shape[0]

    @pl.when(i == 0)
    def _():
        ybuf[...] = jnp.zeros_like(ybuf)

    def segments(op):
        for e in range(N_EXPERTS):
            gbase = tab_ref[i, TAB_GBASE + e]
            off = tab_ref[i, TAB_SEGOFF + e]
            _for_each_piece(tab_ref[i, TAB_SEGROWS + e], SEG_BITS,
                            lambda pos, size: op(_segment_copy(ys_ref, gbase + pos, ybuf, off + pos, size, sem)))

    segments(lambda c: c.start())
    segments(lambda c: c.wait())

    loc = loc_ref[...]
    lane = lax.broadcasted_iota(jnp.int32, loc.shape, 1)
    scol = lax.broadcasted_iota(jnp.int32, (tm, SORTED_ROWS), 1).astype(F32)
    unperm = jnp.where((scol == _lane_column(loc, lane, 0)) | (scol == _lane_column(loc, lane, 1)), 1.0, 0.0).astype(BF16)
    f = jnp.dot(unperm, ybuf[...], preferred_element_type=F32)
    o_ref[...] = _layer_norm(alpha * x1_ref[...] + f, g_ref[...], b_ref[...])


def _combine(tab, loc, x1, ln_g, ln_b, ys, alpha):
    n, d = x1.shape
    nt = n // TM
    grid_spec = pltpu.PrefetchScalarGridSpec(
        num_scalar_prefetch=1,
        grid=(nt,),
        in_specs=[pl.BlockSpec((TM, LANES), lambda i, tab: (i, 0)),
                  pl.BlockSpec((TM, d), lambda i, tab: (i, 0)),
                  pl.BlockSpec(ln_g.shape, lambda i, tab: (0, 0)),
                  pl.BlockSpec(ln_b.shape, lambda i, tab: (0, 0)),
                  pl.BlockSpec(memory_space=pl.ANY)],
        out_specs=pl.BlockSpec((TM, d), lambda i, tab: (i, 0)),
        scratch_shapes=[pltpu.VMEM((SORTED_ROWS, D_MODEL), BF16),
                        pltpu.SemaphoreType.DMA(())],
    )
    return pl.pallas_call(
        functools.partial(_combine_kernel, alpha),
        grid_spec=grid_spec,
        out_shape=jax.ShapeDtypeStruct((n, d), F32),
        compiler_params=pltpu.CompilerParams(dimension_semantics=("arbitrary",), vmem_limit_bytes=VMEM_LIMIT),
        name="combine",
    )(tab, loc, x1, ln_g, ln_b, ys)


def _rope_tables(seq):
    half = QK_ROPE // 2
    inv = ROPE_BASE ** (-jnp.arange(half, dtype=F32) / half)
    ang = jnp.arange(seq).astype(F32)[:, None] * inv[None, :]
    cos, sin = jnp.cos(ang), jnp.sin(ang)
    cos_t = jnp.tile(cos, (1, 2 * MLA_HEADS))
    sin_t = jnp.tile(jnp.concatenate([-sin, sin], axis=-1), (1, MLA_HEADS))
    return cos_t, sin_t


def _block_diag(w):
    nb, c, dd = w.shape
    eye = jnp.eye(nb, dtype=w.dtype)
    return (eye[:, None, :, None] * w[:, :, None, :]).reshape(nb * c, nb * dd)


def _layer_weights(l, w_in, conv_w, conv_b, w_rgate, b_rgate, w_igate, b_igate, lru_lambda,
                   q_norm_g, w_q_up, kv_norm_g, w_kv_up, out_norm_g, w_out, ln1_g, ln1_b,
                   router_w, router_bias, ln2_g, ln2_b):
    swap = (jnp.arange(QK_ROPE) + QK_ROPE // 2) % QK_ROPE
    o4 = 2 * LRU_WIDTH + Q_LORA + KV_LORA
    kr = w_in[l][:, o4:]
    w_in_l = jnp.concatenate([w_in[l][:, :o4], kr, kr, kr[:, swap], kr[:, swap]], axis=-1)
    wq = w_q_up[l]
    wq_l = jnp.concatenate([wq[:, :, :QK_NOPE].reshape(Q_LORA, -1),
                            wq[:, :, QK_NOPE:].reshape(Q_LORA, -1),
                            wq[:, :, QK_NOPE:][:, :, swap].reshape(Q_LORA, -1)], axis=-1)
    wkv = w_kv_up[l]
    wkv_l = jnp.concatenate([wkv[:, :, :QK_NOPE].reshape(KV_LORA, -1),
                             wkv[:, :, QK_NOPE:].reshape(KV_LORA, -1)], axis=-1)
    row = lambda a: a.reshape(1, -1)
    pad = LANES - N_EXPERTS
    return {
        "w_in": w_in_l.astype(BF16), "conv_w": conv_w[l], "conv_b": row(conv_b[l]),
        "w_r": _block_diag(w_rgate[l]).astype(BF16), "b_r": row(b_rgate[l]),
        "w_i": _block_diag(w_igate[l]).astype(BF16), "b_i": row(b_igate[l]),
        "lam": row(lru_lambda[l]), "q_g": row(q_norm_g[l]), "w_q": wq_l.astype(BF16),
        "kv_g": row(kv_norm_g[l]), "w_kv": wkv_l.astype(BF16),
        "og_lru": row(out_norm_g[l][:LRU_WIDTH]), "og_mla": row(out_norm_g[l][LRU_WIDTH:]),
        "w_out": w_out[l].astype(BF16), "ln1_g": row(ln1_g[l]), "ln1_b": row(ln1_b[l]),
        "router_w": jnp.pad(router_w, ((0, 0), (0, pad))).astype(BF16),
        "router_b": jnp.pad(router_bias, (0, pad)).reshape(1, -1),
        "ln2_g": row(ln2_g[l]), "ln2_b": row(ln2_b[l]),
    }


def _dispatch_plan(counts, n_rows):
    nt = counts.shape[0] // SUBLANES
    cnt = counts.reshape(nt, SUBLANES, LANES)[:, 0, :N_EXPERTS].astype(jnp.int32)
    segrows = (cnt + SEG_ALIGN - 1) // SEG_ALIGN * SEG_ALIGN
    total = segrows.sum(axis=0)
    padded = (total + TE - 1) // TE * TE
    pad_end = jnp.cumsum(padded)
    pad_start = pad_end - padded
    gbase = pad_start[None, :] + jnp.cumsum(segrows, axis=0) - segrows
    segoff = jnp.cumsum(segrows, axis=1) - segrows
    per_tile = lambda a: jnp.broadcast_to(a[None, :], (nt, N_EXPERTS))
    tab = jnp.concatenate([gbase, segoff, segrows, per_tile(pad_start + total), per_tile(padded - total),
                           jnp.broadcast_to(pad_end[-1:][None, :], (nt, 1))], axis=1)
    segoff_f = jnp.zeros((nt, SUBLANES, LANES), F32).at[:, :, :N_EXPERTS].set(segoff.astype(F32)[:, None, :])
    tile_row = jnp.arange(n_rows // TE, dtype=jnp.int32) * TE
    tile_expert = jnp.minimum(jnp.sum(tile_row[:, None] >= pad_end[None, :], axis=1), N_EXPERTS - 1)
    n_valid = pad_end[-1:] // TE
    return (tab.astype(jnp.int32), segoff_f.reshape(nt * SUBLANES, LANES),
            tile_expert.astype(jnp.int32), n_valid.astype(jnp.int32))


def kernel(x, w_in, conv_w, conv_b, w_rgate, b_rgate, w_igate, b_igate, lru_lambda, q_norm_g, w_q_up, kv_norm_g, w_kv_up, out_norm_g, w_out, ln1_g, ln1_b, router_w, router_bias, w_exp_gate, w_exp_up, w_exp_down, ln2_g, ln2_b):
    bsz, seq, d = x.shape
    depth = w_in.shape[0]
    assert d == D_MODEL and seq % TM == 0 and seq % TQ == 0 and TQ == TK
    alpha = float((2 * depth) ** 0.25)
    n = bsz * seq
    n_rows = n * TOP_K + N_EXPERTS * SEG_ALIGN * (n // TM) + N_EXPERTS * TE
    cos_t, sin_t = _rope_tables(seq)
    for l in range(depth):
        w = _layer_weights(l, w_in, conv_w, conv_b, w_rgate, b_rgate, w_igate, b_igate, lru_lambda,
                           q_norm_g, w_q_up, kv_norm_g, w_kv_up, out_norm_g, w_out, ln1_g, ln1_b,
                           router_w, router_bias, ln2_g, ln2_b)
        mixl, q, k, v = _proj(x, w, cos_t, sin_t)
        mixm = _attention(q, k, v, w["og_mla"])
        x1, xb, route, counts = _outproj(x.reshape(n, d), mixl.reshape(n, -1), mixm.reshape(n, -1), w, alpha)
        tab, segoff, tile_expert, n_valid = _dispatch_plan(counts, n_rows)
        xs, gs, loc = _scatter(tab, route, segoff, xb, n_rows)
        ys = _experts(l, tile_expert, n_valid, xs, gs, w_exp_gate, w_exp_up, w_exp_down)
        x = _combine(tab, loc, x1, w["ln2_g"], w["ln2_b"], ys, alpha).reshape(bsz, seq, d)
    return x
```
